```python
import jax, jax.numpy as jnp
from jax import lax
import numpy as np

D_MODEL = 1024
BATCH = 1
SEQ = 16384
DEPTH = 1

HEAD_DIM = 64
FOX_HEADS = 8
SWA_Q_HEADS = 8
SWA_KV_HEADS = 2
SWA_GROUP = SWA_Q_HEADS // SWA_KV_HEADS
WINDOW = 128
Q_BLOCK = 128
D_FF = 2816
ROPE_THETA = 10000.0
RMS_EPS = 1e-6
FOX_WIDTH = FOX_HEADS * HEAD_DIM
SWA_Q_WIDTH = SWA_Q_HEADS * HEAD_DIM
SWA_KV_WIDTH = SWA_KV_HEADS * HEAD_DIM
IN_SPLIT_SIZES = (FOX_WIDTH, FOX_WIDTH, FOX_WIDTH, FOX_HEADS, SWA_Q_WIDTH, SWA_KV_WIDTH, SWA_KV_WIDTH, D_MODEL, D_MODEL)
IN_COLS = sum(IN_SPLIT_SIZES)

kernel_name = "hybrid_fox_swa_sink_macaron"


def rms_norm(x, g):
    xf = x.astype(jnp.float32)
    y = xf * lax.rsqrt(jnp.mean(xf * xf, axis=-1, keepdims=True) + RMS_EPS) * g.astype(jnp.float32)
    return y.astype(x.dtype)


def swiglu(h, w_gate, w_up, w_down):
    return (jax.nn.silu(h @ w_gate) * (h @ w_up)) @ w_down


def rope_tables(seq_len, dtype):
    inv_freq = 1.0 / (ROPE_THETA ** (jnp.arange(0, HEAD_DIM, 2, dtype=jnp.float32) / HEAD_DIM))
    ang = jnp.arange(seq_len, dtype=jnp.float32)[:, None] * inv_freq[None, :]
    return jnp.cos(ang).astype(dtype), jnp.sin(ang).astype(dtype)


def apply_rope(x, cos, sin):
    x1, x2 = jnp.split(x, 2, axis=-1)
    c = cos[None, :, None, :]
    s = sin[None, :, None, :]
    return jnp.concatenate([x1 * c - x2 * s, x2 * c + x1 * s], axis=-1)


def fox_attention(q, k, v, log_f):
    B, S, H, dh = q.shape
    nb = S // Q_BLOCK
    scale = dh ** -0.5
    cum = jnp.cumsum(log_f, axis=1).transpose(0, 2, 1)
    qb = q.reshape(B, nb, Q_BLOCK, H, dh).transpose(1, 0, 2, 3, 4)
    cb = cum.reshape(B, H, nb, Q_BLOCK).transpose(2, 0, 1, 3)
    key_pos = jnp.arange(S)

    def block(args):
        i, q_i, c_i = args
        s = jnp.einsum('bqhd,bkhd->bhqk', q_i, k).astype(jnp.float32) * scale
        s = s + (c_i[..., :, None] - cum[:, :, None, :])
        q_pos = i * Q_BLOCK + jnp.arange(Q_BLOCK)
        causal = key_pos[None, :] <= q_pos[:, None]
        s = jnp.where(causal, s, -jnp.inf)
        p = jax.nn.softmax(s, axis=-1).astype(v.dtype)
        return jnp.einsum('bhqk,bkhd->bqhd', p, v)

    out = lax.map(block, (jnp.arange(nb), qb, cb))
    return out.transpose(1, 0, 2, 3, 4).reshape(B, S, H * dh)


def swa_sink_attention(q, k, v, sinks):
    B, S, Hq, dh = q.shape
    nb = S // WINDOW
    scale = dh ** -0.5
    qb = q.reshape(B, nb, WINDOW, SWA_KV_HEADS, SWA_GROUP, dh)

    def band(t):
        tb = t.reshape(B, nb, WINDOW, SWA_KV_HEADS, dh)
        prev = jnp.pad(tb, ((0, 0), (1, 0), (0, 0), (0, 0), (0, 0)))[:, :-1]
        return jnp.concatenate([prev, tb], axis=2)

    kb, vb = band(k), band(v)
    s = jnp.einsum('bnqhgd,bnkhd->bnhgqk', qb, kb).astype(jnp.float32) * scale
    q_off = jnp.arange(WINDOW)[:, None] + WINDOW
    k_off = jnp.arange(2 * WINDOW)[None, :]
    diff = q_off - k_off
    in_window = (diff >= 0) & (diff < WINDOW)
    exists = (jnp.arange(nb)[:, None] * WINDOW + k_off - WINDOW) >= 0
    mask = in_window[None] & exists[:, None, :]
    s = jnp.where(mask[None, :, None, None], s, -jnp.inf)
    sink = jnp.broadcast_to(sinks.astype(jnp.float32).reshape(1, 1, SWA_KV_HEADS, SWA_GROUP, 1, 1), s.shape[:-1] + (1,))
    p = jax.nn.softmax(jnp.concatenate([s, sink], axis=-1), axis=-1)[..., :-1].astype(v.dtype)
    out = jnp.einsum('bnhgqk,bnkhd->bnqhgd', p, vb)
    return out.reshape(B, S, Hq * dh)


def hybrid_mixer(h, w_in, forget_bias, sinks, w_branch_fox, w_branch_swa, w_out, cos, sin):
    B, S, _ = h.shape
    proj = h @ w_in
    cuts = [int(c) for c in np.cumsum(IN_SPLIT_SIZES)[:-1]]
    q_a, k_a, v_a, f_a, q_b, k_b, v_b, gate_a, gate_b = jnp.split(proj, cuts, axis=-1)
    log_f = jax.nn.log_sigmoid(f_a.astype(jnp.float32) + forget_bias.astype(jnp.float32))
    o_a = fox_attention(q_a.reshape(B, S, FOX_HEADS, HEAD_DIM), k_a.reshape(B, S, FOX_HEADS, HEAD_DIM),
                        v_a.reshape(B, S, FOX_HEADS, HEAD_DIM), log_f)
    qr = apply_rope(q_b.reshape(B, S, SWA_Q_HEADS, HEAD_DIM), cos, sin)
    kr = apply_rope(k_b.reshape(B, S, SWA_KV_HEADS, HEAD_DIM), cos, sin)
    o_b = swa_sink_attention(qr, kr, v_b.reshape(B, S, SWA_KV_HEADS, HEAD_DIM), sinks)
    merged = jax.nn.sigmoid(gate_a) * (o_a @ w_branch_fox) + jax.nn.sigmoid(gate_b) * (o_b @ w_branch_swa)
    return merged @ w_out


def setup_inputs(seed: int = 0) -> dict:
    key = jax.random.key(seed)
    ks = jax.random.split(key, 20)
    f32 = jnp.float32

    def dense(k, fan_in, fan_out):
        return jax.random.normal(k, (DEPTH, fan_in, fan_out), f32) * fan_in ** -0.5

    def gain(k):
        return 1.0 + 0.05 * jax.random.normal(k, (DEPTH, D_MODEL), f32)

    return {
        "x": jax.random.normal(ks[0], (BATCH, SEQ, D_MODEL), f32),
        "ffn1_pre_g": gain(ks[1]),
        "ffn1_post_g": gain(ks[2]),
        "ffn1_w_gate": dense(ks[3], D_MODEL, D_FF),
        "ffn1_w_up": dense(ks[4], D_MODEL, D_FF),
        "ffn1_w_down": dense(ks[5], D_FF, D_MODEL),
        "mix_pre_g": gain(ks[6]),
        "mix_post_g": gain(ks[7]),
        "w_in": dense(ks[8], D_MODEL, IN_COLS),
        "fox_forget_bias": 1.0 + 3.0 * jax.random.uniform(ks[9], (DEPTH, FOX_HEADS), f32),
        "swa_sinks": 0.5 * jax.random.normal(ks[10], (DEPTH, SWA_Q_HEADS), f32),
        "w_branch_fox": dense(ks[11], FOX_WIDTH, D_MODEL),
        "w_branch_swa": dense(ks[12], SWA_Q_WIDTH, D_MODEL),
        "w_out": dense(ks[13], D_MODEL, D_MODEL),
        "ffn2_pre_g": gain(ks[14]),
        "ffn2_post_g": gain(ks[15]),
        "ffn2_w_gate": dense(ks[16], D_MODEL, D_FF),
        "ffn2_w_up": dense(ks[17], D_MODEL, D_FF),
        "ffn2_w_down": dense(ks[18], D_FF, D_MODEL),
    }


def reference(x, ffn1_pre_g, ffn1_post_g, ffn1_w_gate, ffn1_w_up, ffn1_w_down,
              mix_pre_g, mix_post_g, w_in, fox_forget_bias, swa_sinks,
              w_branch_fox, w_branch_swa, w_out,
              ffn2_pre_g, ffn2_post_g, ffn2_w_gate, ffn2_w_up, ffn2_w_down):
    S = x.shape[1]
    cos, sin = rope_tables(S, x.dtype)
    for l in range(DEPTH):
        h = swiglu(rms_norm(x, ffn1_pre_g[l]), ffn1_w_gate[l], ffn1_w_up[l], ffn1_w_down[l])
        x = x + 0.5 * rms_norm(h, ffn1_post_g[l])
        h = hybrid_mixer(rms_norm(x, mix_pre_g[l]), w_in[l], fox_forget_bias[l], swa_sinks[l],
                         w_branch_fox[l], w_branch_swa[l], w_out[l], cos, sin)
        x = x + rms_norm(h, mix_post_g[l])
        h = swiglu(rms_norm(x, ffn2_pre_g[l]), ffn2_w_gate[l], ffn2_w_up[l], ffn2_w_down[l])
        x = x + 0.5 * rms_norm(h, ffn2_post_g[l])
    return x
```

```python
import functools

import jax
import jax.numpy as jnp
from jax import lax
from jax.experimental import pallas as pl
from jax.experimental.pallas import tpu as pltpu

D_MODEL = 1024
SEQ = 16384
HEAD_DIM = 64
FOX_HEADS = 8
SWA_Q_HEADS = 8
SWA_KV_HEADS = 2
SWA_GROUP = SWA_Q_HEADS // SWA_KV_HEADS
WINDOW = 128
D_FF = 2816
ROPE_THETA = 10000.0
RMS_EPS = 1e-6
FOX_WIDTH = FOX_HEADS * HEAD_DIM
SWA_Q_WIDTH = SWA_Q_HEADS * HEAD_DIM
SWA_KV_WIDTH = SWA_KV_HEADS * HEAD_DIM

LANES = 128
VMEM_LIMIT_BYTES = 56 * 1024 * 1024

ROW_TILE = 512
FF_CHUNK = 256
FOX_TQ = 512
FOX_TK = 256
FOX_QK_DEPTH = 128
FOX_V_ROWS = 80
SWA_ROWS = 512
MASKED = -1e30

F32 = jnp.float32
BF16 = jnp.bfloat16


def _rms(x, g):
    return x * lax.rsqrt(jnp.mean(x * x, axis=-1, keepdims=True) + RMS_EPS) * g


def _sigmoid(x):
    return 0.5 * (1.0 + jnp.tanh(0.5 * x))


def _dot(a, b):
    return jnp.dot(a, b, preferred_element_type=F32)


def _swiglu(xn, wg_ref, wu_ref, wd_ref):
    acc = None
    for c in range(D_FF // FF_CHUNK):
        sl = slice(c * FF_CHUNK, (c + 1) * FF_CHUNK)
        g = _dot(xn, wg_ref[:, sl])
        u = _dot(xn, wu_ref[:, sl])
        h = (g * _sigmoid(g) * u).astype(BF16)
        y = _dot(h, wd_ref[sl, :])
        acc = y if acc is None else acc + y
    return acc


def _const_spec(shape):
    return pl.BlockSpec(shape, lambda *_: (0,) * len(shape), pipeline_mode=pl.Buffered(1))


def _row_spec(cols, rows=ROW_TILE):
    return pl.BlockSpec((rows, cols), lambda i: (i, 0))


def _ffn1_kernel(x_ref, gpre_ref, gpost_ref, wg_ref, wu_ref, wd_ref, o_ref):
    x = x_ref[...]
    xn = _rms(x, gpre_ref[...]).astype(BF16)
    h = _swiglu(xn, wg_ref, wu_ref, wd_ref)
    o_ref[...] = x + 0.5 * _rms(h, gpost_ref[...])


def _ffn1(x, gpre, gpost, wg, wu, wd):
    return pl.pallas_call(
        _ffn1_kernel,
        grid=(SEQ // ROW_TILE,),
        in_specs=[
            _row_spec(D_MODEL),
            _const_spec((1, D_MODEL)),
            _const_spec((1, D_MODEL)),
            _const_spec((D_MODEL, D_FF)),
            _const_spec((D_MODEL, D_FF)),
            _const_spec((D_FF, D_MODEL)),
        ],
        out_specs=_row_spec(D_MODEL),
        out_shape=jax.ShapeDtypeStruct((SEQ, D_MODEL), F32),
        compiler_params=pltpu.CompilerParams(
            dimension_semantics=("arbitrary",), vmem_limit_bytes=VMEM_LIMIT_BYTES
        ),
        name="ffn1",
    )(x, gpre, gpost, wg, wu, wd)


def _split3(x):
    hi = x.astype(BF16)
    r = x - hi.astype(F32)
    mid = r.astype(BF16)
    lo = (r - mid.astype(F32)).astype(BF16)
    return hi, mid, lo


def _proj_kernel(
    x_ref, g_ref, wa_ref, wf_ref, fb_ref, wb_ref, wgate_ref, cos_ref, sin_ref,
    qa_ref, ka_ref, va_ref, c_ref, qb_ref, kb_ref, vb_ref, ga_ref, gb_ref, carry_ref,
):
    @pl.when(pl.program_id(0) == 0)
    def _():
        carry_ref[...] = jnp.zeros_like(carry_ref)

    hn = _rms(x_ref[...], g_ref[...]).astype(BF16)

    pa = _dot(hn, wa_ref[...])
    qa_ref[...] = (pa[:, :FOX_WIDTH] * HEAD_DIM**-0.5).astype(BF16)
    ka_ref[...] = pa[:, FOX_WIDTH : 2 * FOX_WIDTH].astype(BF16)
    va_ref[...] = pa[:, 2 * FOX_WIDTH :].astype(BF16)

    f = _dot(hn, wf_ref[...]) + fb_ref[...]
    log_f = jnp.minimum(f, 0.0) - jnp.log1p(jnp.exp(-jnp.abs(f)))
    hi, mid, lo = _split3(log_f)
    lane = lax.broadcasted_iota(jnp.int32, log_f.shape, 1)
    parts = jnp.where(lane < 8, hi, jnp.where(lane < 16, mid, jnp.where(lane < 24, lo, jnp.zeros_like(lo))))
    rows = log_f.shape[0]
    r_i = lax.broadcasted_iota(jnp.int32, (rows, rows), 0)
    c_i = lax.broadcasted_iota(jnp.int32, (rows, rows), 1)
    tril = jnp.where(r_i >= c_i, 1.0, 0.0).astype(BF16)
    y = _dot(tril, parts)
    csum = y + pltpu.roll(y, LANES - 8, 1) + pltpu.roll(y, LANES - 16, 1)
    c = jnp.where(lane < 8, csum + carry_ref[...], 0.0)
    carry_ref[...] = c[rows - 1 : rows, :]
    c_hi, c_mid, c_lo = _split3(c + pltpu.roll(c, 8, 1) + pltpu.roll(c, 16, 1))
    c_ref[...] = jnp.where(lane < 8, c_hi, jnp.where(lane < 16, c_mid, c_lo))

    pb = _dot(hn, wb_ref[...])
    cos = cos_ref[...]
    sin = sin_ref[...]
    lane_b = lax.broadcasted_iota(jnp.int32, cos.shape, 1)
    first_half = (lane_b % HEAD_DIM) < (HEAD_DIM // 2)

    def rope(t):
        partner = jnp.where(first_half, pltpu.roll(t, LANES - 32, 1), pltpu.roll(t, 32, 1))
        return t * cos + partner * sin

    for j in range(SWA_Q_WIDTH // LANES):
        sl = slice(j * LANES, (j + 1) * LANES)
        qb_ref[:, sl] = (rope(pb[:, sl]) * HEAD_DIM**-0.5).astype(BF16)
    kb_ref[...] = rope(pb[:, SWA_Q_WIDTH : SWA_Q_WIDTH + SWA_KV_WIDTH]).astype(BF16)
    vb_ref[...] = pb[:, SWA_Q_WIDTH + SWA_KV_WIDTH :].astype(BF16)

    pg = _dot(hn, wgate_ref[...])
    sg = _sigmoid(pg)
    ga_ref[...] = sg[:, :D_MODEL].astype(BF16)
    gb_ref[...] = sg[:, D_MODEL:].astype(BF16)


def _proj(x1, g, wa, wf, fb, wb, wgate, cos, sin):
    out_cols = (FOX_WIDTH, FOX_WIDTH, FOX_WIDTH, LANES, SWA_Q_WIDTH, SWA_KV_WIDTH, SWA_KV_WIDTH, D_MODEL, D_MODEL)
    out_dtypes = (BF16,) * len(out_cols)
    return pl.pallas_call(
        _proj_kernel,
        grid=(SEQ // ROW_TILE,),
        in_specs=[
            _row_spec(D_MODEL),
            _const_spec((1, D_MODEL)),
            _const_spec(wa.shape),
            _const_spec(wf.shape),
            _const_spec(fb.shape),
            _const_spec(wb.shape),
            _const_spec(wgate.shape),
            _row_spec(LANES),
            _row_spec(LANES),
        ],
        out_specs=[_row_spec(c) for c in out_cols],
        out_shape=[jax.ShapeDtypeStruct((SEQ, c), d) for c, d in zip(out_cols, out_dtypes)],
        scratch_shapes=[pltpu.VMEM((1, LANES), F32)],
        compiler_params=pltpu.CompilerParams(
            dimension_semantics=("arbitrary",), vmem_limit_bytes=VMEM_LIMIT_BYTES
        ),
        name="proj",
    )(x1, g, wa, wf, fb, wb, wgate, cos, sin)


def _fox_kernel(qt_ref, ka_ref, vt_ref, o_ref, acc_ref, m_ref):
    i = pl.program_id(1)
    acc_ref[...] = jnp.zeros_like(acc_ref)
    m_ref[...] = jnp.full_like(m_ref, MASKED)

    def step(j, c0, key_offset):
        q = qt_ref[:, c0:]
        s = _dot(ka_ref[j], q)
        if key_offset is not None:
            kpos = lax.broadcasted_iota(jnp.int32, s.shape, 0) + key_offset
            qpos = lax.broadcasted_iota(jnp.int32, s.shape, 1) + c0
            s = jnp.where(kpos <= qpos, s, MASKED)
        m_old = m_ref[:, c0:]
        m_new = jnp.maximum(m_old, jnp.max(s, axis=0, keepdims=True))
        alpha = jnp.exp(m_old - m_new)
        p = jnp.exp(s - m_new).astype(BF16)
        acc_ref[:, c0:] = alpha * acc_ref[:, c0:] + _dot(vt_ref[j], p)
        m_ref[:, c0:] = m_new

    chunks_per_tile = FOX_TQ // FOX_TK
    n_full = i * chunks_per_tile

    def body(j, carry):
        step(j, 0, None)
        return carry

    lax.fori_loop(0, n_full, body, 0)
    for d in range(chunks_per_tile):
        step(n_full + d, d * FOX_TK, d * FOX_TK)

    acc = acc_ref[...]
    o_ref[...] = (acc[:HEAD_DIM, :] / acc[HEAD_DIM : HEAD_DIM + 1, :]).astype(o_ref.dtype)


def _fox(qt, ka, vt):
    nk = SEQ // FOX_TK
    return pl.pallas_call(
        _fox_kernel,
        grid=(FOX_HEADS, SEQ // FOX_TQ),
        in_specs=[
            pl.BlockSpec((None, FOX_QK_DEPTH, FOX_TQ), lambda h, i: (h, 0, i)),
            pl.BlockSpec((None, nk, FOX_TK, FOX_QK_DEPTH), lambda h, i: (h, 0, 0, 0)),
            pl.BlockSpec((None, nk, FOX_V_ROWS, FOX_TK), lambda h, i: (h, 0, 0, 0)),
        ],
        out_specs=pl.BlockSpec((None, HEAD_DIM, FOX_TQ), lambda h, i: (h, 0, i)),
        out_shape=jax.ShapeDtypeStruct((FOX_HEADS, HEAD_DIM, SEQ), BF16),
        scratch_shapes=[pltpu.VMEM((FOX_V_ROWS, FOX_TQ), F32), pltpu.VMEM((1, FOX_TQ), F32)],
        compiler_params=pltpu.CompilerParams(
            dimension_semantics=("arbitrary", "arbitrary"), vmem_limit_bytes=VMEM_LIMIT_BYTES
        ),
        name="fox",
    )(qt, ka, vt)


def _swa_kernel(sink_ref, q_ref, k_ref, v_ref, kprev_ref, vprev_ref, o_ref):
    step_i = pl.program_id(0)
    q_i = lax.broadcasted_iota(jnp.int32, (WINDOW, 2 * WINDOW), 0) + WINDOW
    k_i = lax.broadcasted_iota(jnp.int32, (WINDOW, 2 * WINDOW), 1)
    diff = q_i - k_i
    in_window = (diff >= 0) & (diff < WINDOW)
    for b in range(SWA_ROWS // WINDOW):
        rows = slice(b * WINDOW, (b + 1) * WINDOW)
        if b == 0:
            k_prev, v_prev = kprev_ref[...], vprev_ref[...]
            mask = in_window & ((k_i >= WINDOW) | (step_i > 0))
        else:
            prev = slice((b - 1) * WINDOW, b * WINDOW)
            k_prev, v_prev = k_ref[prev, :], v_ref[prev, :]
            mask = in_window
        k_cat = jnp.concatenate([k_prev, k_ref[rows, :]], axis=0)
        v_cat = jnp.concatenate([v_prev, v_ref[rows, :]], axis=0)
        outs = []
        for h in range(SWA_Q_HEADS):
            g = h // SWA_GROUP
            q_h = q_ref[rows, h * HEAD_DIM : (h + 1) * HEAD_DIM]
            k_g = k_cat[:, g * HEAD_DIM : (g + 1) * HEAD_DIM]
            v_g = v_cat[:, g * HEAD_DIM : (g + 1) * HEAD_DIM]
            s = lax.dot_general(q_h, k_g, (((1,), (1,)), ((), ())), preferred_element_type=F32)
            s = jnp.where(mask, s, MASKED)
            sink = sink_ref[h]
            m = jnp.maximum(jnp.max(s, axis=-1, keepdims=True), sink)
            p = jnp.exp(s - m)
            denom = jnp.sum(p, axis=-1, keepdims=True) + jnp.exp(sink - m)
            outs.append(_dot(p.astype(BF16), v_g) / denom)
        o_ref[rows, :] = jnp.concatenate(outs, axis=-1).astype(o_ref.dtype)


def _swa(sinks, qb, kb, vb):
    blocks_per_step = SWA_ROWS // WINDOW
    prev_spec = pl.BlockSpec((WINDOW, SWA_KV_WIDTH), lambda i: (jnp.maximum(i * blocks_per_step - 1, 0), 0))
    return pl.pallas_call(
        _swa_kernel,
        grid=(SEQ // SWA_ROWS,),
        in_specs=[
            pl.BlockSpec(memory_space=pltpu.SMEM),
            _row_spec(SWA_Q_WIDTH, SWA_ROWS),
            _row_spec(SWA_KV_WIDTH, SWA_ROWS),
            _row_spec(SWA_KV_WIDTH, SWA_ROWS),
            prev_spec,
            prev_spec,
        ],
        out_specs=_row_spec(SWA_Q_WIDTH, SWA_ROWS),
        out_shape=jax.ShapeDtypeStruct((SEQ, SWA_Q_WIDTH), BF16),
        compiler_params=pltpu.CompilerParams(
            dimension_semantics=("arbitrary",), vmem_limit_bytes=VMEM_LIMIT_BYTES
        ),
        name="swa",
    )(sinks, qb, kb, vb, kb, vb)


def _out_kernel(
    x_ref, oa_ref, ob_ref, ga_ref, gb_ref, wfox_ref, wswa_ref, wout_ref,
    gmix_ref, gpre_ref, gpost_ref, wg_ref, wu_ref, wd_ref, o_ref,
):
    a = _dot(oa_ref[...], wfox_ref[...])
    b = _dot(ob_ref[...], wswa_ref[...])
    merged = (ga_ref[...].astype(F32) * a + gb_ref[...].astype(F32) * b).astype(BF16)
    h = _dot(merged, wout_ref[...])
    x2 = x_ref[...] + _rms(h, gmix_ref[...])
    xn = _rms(x2, gpre_ref[...]).astype(BF16)
    h2 = _swiglu(xn, wg_ref, wu_ref, wd_ref)
    o_ref[...] = x2 + 0.5 * _rms(h2, gpost_ref[...])


def _out(x1, oa, ob, ga, gb, wfox, wswa, wout, gmix, gpre, gpost, wg, wu, wd):
    return pl.pallas_call(
        _out_kernel,
        grid=(SEQ // ROW_TILE,),
        in_specs=[
            _row_spec(D_MODEL),
            _row_spec(FOX_WIDTH),
            _row_spec(SWA_Q_WIDTH),
            _row_spec(D_MODEL),
            _row_spec(D_MODEL),
            _const_spec(wfox.shape),
            _const_spec(wswa.shape),
            _const_spec(wout.shape),
            _const_spec((1, D_MODEL)),
            _const_spec((1, D_MODEL)),
            _const_spec((1, D_MODEL)),
            _const_spec((D_MODEL, D_FF)),
            _const_spec((D_MODEL, D_FF)),
            _const_spec((D_FF, D_MODEL)),
        ],
        out_specs=_row_spec(D_MODEL),
        out_shape=jax.ShapeDtypeStruct((SEQ, D_MODEL), F32),
        compiler_params=pltpu.CompilerParams(
            dimension_semantics=("arbitrary",), vmem_limit_bytes=VMEM_LIMIT_BYTES
        ),
        name="out_ffn2",
    )(x1, oa, ob, ga, gb, wfox, wswa, wout, gmix, gpre, gpost, wg, wu, wd)


def _rope_tables():
    inv_freq = 1.0 / (ROPE_THETA ** (jnp.arange(0, HEAD_DIM, 2, dtype=F32) / HEAD_DIM))
    ang = jnp.arange(SEQ, dtype=F32)[:, None] * inv_freq[None, :]
    cos, sin = jnp.cos(ang), jnp.sin(ang)
    reps = LANES // HEAD_DIM
    return jnp.tile(jnp.concatenate([cos, cos], axis=1), (1, reps)), jnp.tile(jnp.concatenate([-sin, sin], axis=1), (1, reps))


def _fox_layouts(qa, ka, va, c):
    nk = SEQ // FOX_TK
    c_parts = c[:, : 3 * FOX_HEADS].reshape(SEQ, 3, FOX_HEADS).transpose(2, 0, 1)
    ones3 = jnp.ones((FOX_HEADS, SEQ, 3), BF16)
    pad = jnp.zeros((FOX_HEADS, SEQ, FOX_QK_DEPTH - HEAD_DIM - 6), BF16)
    heads = lambda t: t.reshape(SEQ, FOX_HEADS, HEAD_DIM).transpose(1, 0, 2)
    k_aug = jnp.concatenate([heads(ka), ones3, -c_parts, pad], axis=-1)
    q_aug = jnp.concatenate([heads(qa), c_parts, ones3, pad], axis=-1)
    v_aug = jnp.concatenate(
        [heads(va), jnp.ones((FOX_HEADS, SEQ, 1), BF16), jnp.zeros((FOX_HEADS, SEQ, FOX_V_ROWS - HEAD_DIM - 1), BF16)],
        axis=-1,
    )
    qt = q_aug.transpose(0, 2, 1)
    ka_t = k_aug.reshape(FOX_HEADS, nk, FOX_TK, FOX_QK_DEPTH)
    vt = v_aug.reshape(FOX_HEADS, nk, FOX_TK, FOX_V_ROWS).transpose(0, 1, 3, 2)
    return qt, ka_t, vt


def kernel(x, ffn1_pre_g, ffn1_post_g, ffn1_w_gate, ffn1_w_up, ffn1_w_down, mix_pre_g, mix_post_g, w_in, fox_forget_bias, swa_sinks, w_branch_fox, w_branch_swa, w_out, ffn2_pre_g, ffn2_post_g, ffn2_w_gate, ffn2_w_up, ffn2_w_down):
    assert x.shape == (1, SEQ, D_MODEL) and ffn1_pre_g.shape[0] == 1
    bf = lambda w: w[0].astype(BF16)
    row = lambda g: g[0].reshape(1, -1)
    xs = x[0]

    x1 = _ffn1(xs, row(ffn1_pre_g), row(ffn1_post_g), bf(ffn1_w_gate), bf(ffn1_w_up), bf(ffn1_w_down))

    w = w_in[0]
    o_f = 3 * FOX_WIDTH
    o_b = o_f + FOX_HEADS
    o_g = o_b + SWA_Q_WIDTH + 2 * SWA_KV_WIDTH
    wa = w[:, :o_f].astype(BF16)
    wf3 = jnp.tile(w[:, o_f:o_b], (1, 3))
    wf = jnp.pad(wf3, ((0, 0), (0, LANES - 3 * FOX_HEADS))).astype(BF16)
    fb = jnp.pad(jnp.tile(fox_forget_bias[0], 3), (0, LANES - 3 * FOX_HEADS)).reshape(1, LANES)
    wb = w[:, o_b:o_g].astype(BF16)
    wgate = w[:, o_g:].astype(BF16)
    cos, sin = _rope_tables()
    qa, ka, va, c, qb, kb, vb, ga, gb = _proj(x1, row(mix_pre_g), wa, wf, fb, wb, wgate, cos, sin)

    qt, ka_t, vt = _fox_layouts(qa, ka, va, c)
    oa_t = _fox(qt, ka_t, vt)
    oa = oa_t.transpose(2, 0, 1).reshape(SEQ, FOX_WIDTH)

    ob = _swa(swa_sinks[0], qb, kb, vb)

    out = _out(
        x1, oa, ob, ga, gb, bf(w_branch_fox), bf(w_branch_swa), bf(w_out),
        row(mix_post_g), row(ffn2_pre_g), row(ffn2_post_g), bf(ffn2_w_gate), bf(ffn2_w_up), bf(ffn2_w_down),
    )
    return out[None]
```

```python
import functools

import jax
import jax.numpy as jnp
from jax import lax
from jax.experimental import pallas as pl
from jax.experimental.pallas import tpu as pltpu

D_MODEL = 1024
SEQ = 16384
HEAD_DIM = 64
FOX_HEADS = 8
SWA_Q_HEADS = 8
SWA_KV_HEADS = 2
SWA_GROUP = SWA_Q_HEADS // SWA_KV_HEADS
WINDOW = 128
D_FF = 2816
ROPE_THETA = 10000.0
RMS_EPS = 1e-6
FOX_WIDTH = FOX_HEADS * HEAD_DIM
SWA_Q_WIDTH = SWA_Q_HEADS * HEAD_DIM
SWA_KV_WIDTH = SWA_KV_HEADS * HEAD_DIM

LANES = 128
VMEM_LIMIT_BYTES = 56 * 1024 * 1024

ROW_TILE = 512
FF_CHUNK = 256
FOX_TQ = 1024
FOX_TK = 512
assert FOX_TQ == 2 * FOX_TK
FOX_QK_DEPTH = 128
FOX_V_ROWS = 80
SWA_ROWS = 512
MASKED = -1e30

F32 = jnp.float32
BF16 = jnp.bfloat16


def _rms(x, g):
    return x * lax.rsqrt(jnp.mean(x * x, axis=-1, keepdims=True) + RMS_EPS) * g


def _sigmoid(x):
    return 0.5 * (1.0 + jnp.tanh(0.5 * x))


def _dot(a, b):
    return jnp.dot(a, b, preferred_element_type=F32)


def _swiglu(xn, wg_ref, wu_ref, wd_ref):
    acc = None
    for c in range(D_FF // FF_CHUNK):
        sl = slice(c * FF_CHUNK, (c + 1) * FF_CHUNK)
        g = _dot(xn, wg_ref[:, sl])
        u = _dot(xn, wu_ref[:, sl])
        h = (g * _sigmoid(g) * u).astype(BF16)
        y = _dot(h, wd_ref[sl, :])
        acc = y if acc is None else acc + y
    return acc


def _const_spec(shape):
    return pl.BlockSpec(shape, lambda *_: (0,) * len(shape), pipeline_mode=pl.Buffered(1))


def _row_spec(cols, rows=ROW_TILE):
    return pl.BlockSpec((rows, cols), lambda i: (i, 0))


def _ffn1_kernel(x_ref, gpre_ref, gpost_ref, wg_ref, wu_ref, wd_ref, o_ref):
    x = x_ref[...]
    xn = _rms(x, gpre_ref[...]).astype(BF16)
    h = _swiglu(xn, wg_ref, wu_ref, wd_ref)
    o_ref[...] = x + 0.5 * _rms(h, gpost_ref[...])


def _ffn1(x, gpre, gpost, wg, wu, wd):
    return pl.pallas_call(
        _ffn1_kernel,
        grid=(SEQ // ROW_TILE,),
        in_specs=[
            _row_spec(D_MODEL),
            _const_spec((1, D_MODEL)),
            _const_spec((1, D_MODEL)),
            _const_spec((D_MODEL, D_FF)),
            _const_spec((D_MODEL, D_FF)),
            _const_spec((D_FF, D_MODEL)),
        ],
        out_specs=_row_spec(D_MODEL),
        out_shape=jax.ShapeDtypeStruct((SEQ, D_MODEL), F32),
        compiler_params=pltpu.CompilerParams(
            dimension_semantics=("arbitrary",), vmem_limit_bytes=VMEM_LIMIT_BYTES
        ),
        name="ffn1",
    )(x, gpre, gpost, wg, wu, wd)


def _split3(x):
    hi = x.astype(BF16)
    r = x - hi.astype(F32)
    mid = r.astype(BF16)
    lo = (r - mid.astype(F32)).astype(BF16)
    return hi, mid, lo


def _proj_kernel(
    x_ref, g_ref, wa_ref, wf_ref, fb_ref, wb_ref, wgate_ref, cos_ref, sin_ref,
    qa_ref, ka_ref, va_ref, c_ref, qb_ref, kb_ref, vb_ref, ga_ref, gb_ref, carry_ref,
):
    @pl.when(pl.program_id(0) == 0)
    def _():
        carry_ref[...] = jnp.zeros_like(carry_ref)

    hn = _rms(x_ref[...], g_ref[...]).astype(BF16)

    pa = _dot(hn, wa_ref[...])
    qa_ref[...] = (pa[:, :FOX_WIDTH] * HEAD_DIM**-0.5).astype(BF16)
    ka_ref[...] = pa[:, FOX_WIDTH : 2 * FOX_WIDTH].astype(BF16)
    va_ref[...] = pa[:, 2 * FOX_WIDTH :].astype(BF16)

    f = _dot(hn, wf_ref[...]) + fb_ref[...]
    log_f = jnp.minimum(f, 0.0) - jnp.log1p(jnp.exp(-jnp.abs(f)))
    hi, mid, lo = _split3(log_f)
    lane = lax.broadcasted_iota(jnp.int32, log_f.shape, 1)
    parts = jnp.where(lane < 8, hi, jnp.where(lane < 16, mid, jnp.where(lane < 24, lo, jnp.zeros_like(lo))))
    rows = log_f.shape[0]
    r_i = lax.broadcasted_iota(jnp.int32, (rows, rows), 0)
    c_i = lax.broadcasted_iota(jnp.int32, (rows, rows), 1)
    tril = jnp.where(r_i >= c_i, 1.0, 0.0).astype(BF16)
    y = _dot(tril, parts)
    csum = y + pltpu.roll(y, LANES - 8, 1) + pltpu.roll(y, LANES - 16, 1)
    c = jnp.where(lane < 8, csum + carry_ref[...], 0.0)
    carry_ref[...] = c[rows - 1 : rows, :]
    c_hi, c_mid, c_lo = _split3(c + pltpu.roll(c, 8, 1) + pltpu.roll(c, 16, 1))
    c_ref[...] = jnp.where(lane < 8, c_hi, jnp.where(lane < 16, c_mid, c_lo))

    pb = _dot(hn, wb_ref[...])
    cos = cos_ref[...]
    sin = sin_ref[...]
    lane_b = lax.broadcasted_iota(jnp.int32, cos.shape, 1)
    first_half = (lane_b % HEAD_DIM) < (HEAD_DIM // 2)

    def rope(t):
        partner = jnp.where(first_half, pltpu.roll(t, LANES - 32, 1), pltpu.roll(t, 32, 1))
        return t * cos + partner * sin

    for j in range(SWA_Q_WIDTH // LANES):
        sl = slice(j * LANES, (j + 1) * LANES)
        qb_ref[:, sl] = (rope(pb[:, sl]) * HEAD_DIM**-0.5).astype(BF16)
    kb_ref[...] = rope(pb[:, SWA_Q_WIDTH : SWA_Q_WIDTH + SWA_KV_WIDTH]).astype(BF16)
    vb_ref[...] = pb[:, SWA_Q_WIDTH + SWA_KV_WIDTH :].astype(BF16)

    pg = _dot(hn, wgate_ref[...])
    sg = _sigmoid(pg)
    ga_ref[...] = sg[:, :D_MODEL].astype(BF16)
    gb_ref[...] = sg[:, D_MODEL:].astype(BF16)


def _proj(x1, g, wa, wf, fb, wb, wgate, cos, sin):
    out_cols = (FOX_WIDTH, FOX_WIDTH, FOX_WIDTH, LANES, SWA_Q_WIDTH, SWA_KV_WIDTH, SWA_KV_WIDTH, D_MODEL, D_MODEL)
    out_dtypes = (BF16,) * len(out_cols)
    return pl.pallas_call(
        _proj_kernel,
        grid=(SEQ // ROW_TILE,),
        in_specs=[
            _row_spec(D_MODEL),
            _const_spec((1, D_MODEL)),
            _const_spec(wa.shape),
            _const_spec(wf.shape),
            _const_spec(fb.shape),
            _const_spec(wb.shape),
            _const_spec(wgate.shape),
            _row_spec(LANES),
            _row_spec(LANES),
        ],
        out_specs=[_row_spec(c) for c in out_cols],
        out_shape=[jax.ShapeDtypeStruct((SEQ, c), d) for c, d in zip(out_cols, out_dtypes)],
        scratch_shapes=[pltpu.VMEM((1, LANES), F32)],
        compiler_params=pltpu.CompilerParams(
            dimension_semantics=("arbitrary",), vmem_limit_bytes=VMEM_LIMIT_BYTES
        ),
        name="proj",
    )(x1, g, wa, wf, fb, wb, wgate, cos, sin)


def _fox_kernel(qt_ref, ka_ref, vt_ref, o_ref, acc_ref, m_ref, s0_ref, s1_ref, mc0_ref, mc1_ref):
    i = pl.program_id(1)
    n_full = (FOX_TQ // FOX_TK) * i
    acc_ref[...] = jnp.zeros_like(acc_ref)
    m_ref[...] = jnp.full_like(m_ref, MASKED)
    slots = ((s0_ref, mc0_ref), (s1_ref, mc1_ref))

    def scores(j, slot, c0=0, triangular=False):
        s_ref, mc_ref = slots[slot]
        s = _dot(ka_ref[j], qt_ref[:, c0:])
        if triangular:
            kpos = lax.broadcasted_iota(jnp.int32, (FOX_TK, FOX_TK), 0)
            qpos = lax.broadcasted_iota(jnp.int32, (FOX_TK, FOX_TK), 1)
            tri = jnp.where(kpos <= qpos, s[:, :FOX_TK], MASKED)
            s_ref[:, c0 : c0 + FOX_TK] = tri
            mc_ref[:, c0 : c0 + FOX_TK] = jnp.max(tri, axis=0, keepdims=True)
            if c0 + FOX_TK < FOX_TQ:
                s_ref[:, c0 + FOX_TK :] = s[:, FOX_TK:]
                mc_ref[:, c0 + FOX_TK :] = jnp.max(s[:, FOX_TK:], axis=0, keepdims=True)
        else:
            s_ref[:, c0:] = s
            mc_ref[:, c0:] = jnp.max(s, axis=0, keepdims=True)

    def accumulate(j, slot, c0=0):
        s_ref, mc_ref = slots[slot]
        m_old = m_ref[:, c0:]
        m_new = jnp.maximum(m_old, mc_ref[:, c0:])
        alpha = jnp.exp(m_old - m_new)
        p = jnp.exp(s_ref[:, c0:] - m_new).astype(BF16)
        acc_ref[:, c0:] = alpha * acc_ref[:, c0:] + _dot(vt_ref[j], p)
        m_ref[:, c0:] = m_new

    def diagonal_tail(pending):
        scores(n_full, 0, 0, True)
        if pending is not None:
            accumulate(pending, 1)
        scores(n_full + 1, 1, FOX_TK, True)
        accumulate(n_full, 0)
        accumulate(n_full + 1, 1, FOX_TK)

    @pl.when(i == 0)
    def _():
        diagonal_tail(None)

    @pl.when(i > 0)
    def _():
        scores(0, 0)

        def pair(jj, carry):
            scores(2 * jj + 1, 1)
            accumulate(2 * jj, 0)
            scores(2 * jj + 2, 0)
            accumulate(2 * jj + 1, 1)
            return carry

        lax.fori_loop(0, n_full // 2 - 1, pair, 0)
        scores(n_full - 1, 1)
        accumulate(n_full - 2, 0)
        diagonal_tail(n_full - 1)

    acc = acc_ref[...]
    o_ref[...] = (acc[:HEAD_DIM, :] / acc[HEAD_DIM : HEAD_DIM + 1, :]).astype(o_ref.dtype)


def _fox(qt, ka, vt):
    nk = SEQ // FOX_TK
    return pl.pallas_call(
        _fox_kernel,
        grid=(FOX_HEADS, SEQ // FOX_TQ),
        in_specs=[
            pl.BlockSpec((None, FOX_QK_DEPTH, FOX_TQ), lambda h, i: (h, 0, i)),
            pl.BlockSpec((None, nk, FOX_TK, FOX_QK_DEPTH), lambda h, i: (h, 0, 0, 0)),
            pl.BlockSpec((None, nk, FOX_V_ROWS, FOX_TK), lambda h, i: (h, 0, 0, 0)),
        ],
        out_specs=pl.BlockSpec((None, HEAD_DIM, FOX_TQ), lambda h, i: (h, 0, i)),
        out_shape=jax.ShapeDtypeStruct((FOX_HEADS, HEAD_DIM, SEQ), BF16),
        scratch_shapes=[
            pltpu.VMEM((FOX_V_ROWS, FOX_TQ), F32),
            pltpu.VMEM((1, FOX_TQ), F32),
            pltpu.VMEM((FOX_TK, FOX_TQ), F32),
            pltpu.VMEM((FOX_TK, FOX_TQ), F32),
            pltpu.VMEM((1, FOX_TQ), F32),
            pltpu.VMEM((1, FOX_TQ), F32),
        ],
        compiler_params=pltpu.CompilerParams(
            dimension_semantics=("arbitrary", "arbitrary"), vmem_limit_bytes=VMEM_LIMIT_BYTES
        ),
        name="fox",
    )(qt, ka, vt)


def _swa_kernel(sink_ref, q_ref, k_ref, v_ref, kprev_ref, vprev_ref, o_ref):
    step_i = pl.program_id(0)
    q_i = lax.broadcasted_iota(jnp.int32, (WINDOW, 2 * WINDOW), 0) + WINDOW
    k_i = lax.broadcasted_iota(jnp.int32, (WINDOW, 2 * WINDOW), 1)
    diff = q_i - k_i
    in_window = (diff >= 0) & (diff < WINDOW)
    for b in range(SWA_ROWS // WINDOW):
        rows = slice(b * WINDOW, (b + 1) * WINDOW)
        if b == 0:
            k_prev, v_prev = kprev_ref[...], vprev_ref[...]
            mask = in_window & ((k_i >= WINDOW) | (step_i > 0))
        else:
            prev = slice((b - 1) * WINDOW, b * WINDOW)
            k_prev, v_prev = k_ref[prev, :], v_ref[prev, :]
            mask = in_window
        k_cat = jnp.concatenate([k_prev, k_ref[rows, :]], axis=0)
        v_cat = jnp.concatenate([v_prev, v_ref[rows, :]], axis=0)
        outs = []
        for h in range(SWA_Q_HEADS):
            g = h // SWA_GROUP
            q_h = q_ref[rows, h * HEAD_DIM : (h + 1) * HEAD_DIM]
            k_g = k_cat[:, g * HEAD_DIM : (g + 1) * HEAD_DIM]
            v_g = v_cat[:, g * HEAD_DIM : (g + 1) * HEAD_DIM]
            s = lax.dot_general(q_h, k_g, (((1,), (1,)), ((), ())), preferred_element_type=F32)
            s = jnp.where(mask, s, MASKED)
            sink = sink_ref[h]
            m = jnp.maximum(jnp.max(s, axis=-1, keepdims=True), sink)
            p = jnp.exp(s - m)
            denom = jnp.sum(p, axis=-1, keepdims=True) + jnp.exp(sink - m)
            outs.append(_dot(p.astype(BF16), v_g) / denom)
        o_ref[rows, :] = jnp.concatenate(outs, axis=-1).astype(o_ref.dtype)


def _swa(sinks, qb, kb, vb):
    blocks_per_step = SWA_ROWS // WINDOW
    prev_spec = pl.BlockSpec((WINDOW, SWA_KV_WIDTH), lambda i: (jnp.maximum(i * blocks_per_step - 1, 0), 0))
    return pl.pallas_call(
        _swa_kernel,
        grid=(SEQ // SWA_ROWS,),
        in_specs=[
            pl.BlockSpec(memory_space=pltpu.SMEM),
            _row_spec(SWA_Q_WIDTH, SWA_ROWS),
            _row_spec(SWA_KV_WIDTH, SWA_ROWS),
            _row_spec(SWA_KV_WIDTH, SWA_ROWS),
            prev_spec,
            prev_spec,
        ],
        out_specs=_row_spec(SWA_Q_WIDTH, SWA_ROWS),
        out_shape=jax.ShapeDtypeStruct((SEQ, SWA_Q_WIDTH), BF16),
        compiler_params=pltpu.CompilerParams(
            dimension_semantics=("arbitrary",), vmem_limit_bytes=VMEM_LIMIT_BYTES
        ),
        name="swa",
    )(sinks, qb, kb, vb, kb, vb)


def _out_kernel(
    x_ref, oa_ref, ob_ref, ga_ref, gb_ref, wfox_ref, wswa_ref, wout_ref,
    gmix_ref, gpre_ref, gpost_ref, wg_ref, wu_ref, wd_ref, o_ref,
):
    a = _dot(oa_ref[...], wfox_ref[...])
    b = _dot(ob_ref[...], wswa_ref[...])
    merged = (ga_ref[...].astype(F32) * a + gb_ref[...].astype(F32) * b).astype(BF16)
    h = _dot(merged, wout_ref[...])
    x2 = x_ref[...] + _rms(h, gmix_ref[...])
    xn = _rms(x2, gpre_ref[...]).astype(BF16)
    h2 = _swiglu(xn, wg_ref, wu_ref, wd_ref)
    o_ref[...] = x2 + 0.5 * _rms(h2, gpost_ref[...])


def _out(x1, oa, ob, ga, gb, wfox, wswa, wout, gmix, gpre, gpost, wg, wu, wd):
    return pl.pallas_call(
        _out_kernel,
        grid=(SEQ // ROW_TILE,),
        in_specs=[
            _row_spec(D_MODEL),
            _row_spec(FOX_WIDTH),
            _row_spec(SWA_Q_WIDTH),
            _row_spec(D_MODEL),
            _row_spec(D_MODEL),
            _const_spec(wfox.shape),
            _const_spec(wswa.shape),
            _const_spec(wout.shape),
            _const_spec((1, D_MODEL)),
            _const_spec((1, D_MODEL)),
            _const_spec((1, D_MODEL)),
            _const_spec((D_MODEL, D_FF)),
            _const_spec((D_MODEL, D_FF)),
            _const_spec((D_FF, D_MODEL)),
        ],
        out_specs=_row_spec(D_MODEL),
        out_shape=jax.ShapeDtypeStruct((SEQ, D_MODEL), F32),
        compiler_params=pltpu.CompilerParams(
            dimension_semantics=("arbitrary",), vmem_limit_bytes=VMEM_LIMIT_BYTES
        ),
        name="out_ffn2",
    )(x1, oa, ob, ga, gb, wfox, wswa, wout, gmix, gpre, gpost, wg, wu, wd)


def _rope_tables():
    inv_freq = 1.0 / (ROPE_THETA ** (jnp.arange(0, HEAD_DIM, 2, dtype=F32) / HEAD_DIM))
    ang = jnp.arange(SEQ, dtype=F32)[:, None] * inv_freq[None, :]
    cos, sin = jnp.cos(ang), jnp.sin(ang)
    reps = LANES // HEAD_DIM
    return jnp.tile(jnp.concatenate([cos, cos], axis=1), (1, reps)), jnp.tile(jnp.concatenate([-sin, sin], axis=1), (1, reps))


def _fox_layouts(qa, ka, va, c):
    nk = SEQ // FOX_TK
    c_parts = c[:, : 3 * FOX_HEADS].reshape(SEQ, 3, FOX_HEADS).transpose(2, 0, 1)
    ones3 = jnp.ones((FOX_HEADS, SEQ, 3), BF16)
    pad = jnp.zeros((FOX_HEADS, SEQ, FOX_QK_DEPTH - HEAD_DIM - 6), BF16)
    heads = lambda t: t.reshape(SEQ, FOX_HEADS, HEAD_DIM).transpose(1, 0, 2)
    k_aug = jnp.concatenate([heads(ka), ones3, -c_parts, pad], axis=-1)
    q_aug = jnp.concatenate([heads(qa), c_parts, ones3, pad], axis=-1)
    v_aug = jnp.concatenate(
        [heads(va), jnp.ones((FOX_HEADS, SEQ, 1), BF16), jnp.zeros((FOX_HEADS, SEQ, FOX_V_ROWS - HEAD_DIM - 1), BF16)],
        axis=-1,
    )
    qt = q_aug.transpose(0, 2, 1)
    ka_t = k_aug.reshape(FOX_HEADS, nk, FOX_TK, FOX_QK_DEPTH)
    vt = v_aug.reshape(FOX_HEADS, nk, FOX_TK, FOX_V_ROWS).transpose(0, 1, 3, 2)
    return qt, ka_t, vt


def kernel(x, ffn1_pre_g, ffn1_post_g, ffn1_w_gate, ffn1_w_up, ffn1_w_down, mix_pre_g, mix_post_g, w_in, fox_forget_bias, swa_sinks, w_branch_fox, w_branch_swa, w_out, ffn2_pre_g, ffn2_post_g, ffn2_w_gate, ffn2_w_up, ffn2_w_down):
    assert x.shape == (1, SEQ, D_MODEL) and ffn1_pre_g.shape[0] == 1
    bf = lambda w: w[0].astype(BF16)
    row = lambda g: g[0].reshape(1, -1)
    xs = x[0]

    x1 = _ffn1(xs, row(ffn1_pre_g), row(ffn1_post_g), bf(ffn1_w_gate), bf(ffn1_w_up), bf(ffn1_w_down))

    w = w_in[0]
    o_f = 3 * FOX_WIDTH
    o_b = o_f + FOX_HEADS
    o_g = o_b + SWA_Q_WIDTH + 2 * SWA_KV_WIDTH
    wa = w[:, :o_f].astype(BF16)
    wf3 = jnp.tile(w[:, o_f:o_b], (1, 3))
    wf = jnp.pad(wf3, ((0, 0), (0, LANES - 3 * FOX_HEADS))).astype(BF16)
    fb = jnp.pad(jnp.tile(fox_forget_bias[0], 3), (0, LANES - 3 * FOX_HEADS)).reshape(1, LANES)
    wb = w[:, o_b:o_g].astype(BF16)
    wgate = w[:, o_g:].astype(BF16)
    cos, sin = _rope_tables()
    qa, ka, va, c, qb, kb, vb, ga, gb = _proj(x1, row(mix_pre_g), wa, wf, fb, wb, wgate, cos, sin)

    qt, ka_t, vt = _fox_layouts(qa, ka, va, c)
    oa_t = _fox(qt, ka_t, vt)
    oa = oa_t.transpose(2, 0, 1).reshape(SEQ, FOX_WIDTH)

    ob = _swa(swa_sinks[0], qb, kb, vb)

    out = _out(
        x1, oa, ob, ga, gb, bf(w_branch_fox), bf(w_branch_swa), bf(w_out),
        row(mix_post_g), row(ffn2_pre_g), row(ffn2_post_g), bf(ffn2_w_gate), bf(ffn2_w_up), bf(ffn2_w_down),
    )
    return out[None]
```

```python
import functools

import jax
import jax.numpy as jnp
from jax import lax
from jax.experimental import pallas as pl
from jax.experimental.pallas import tpu as pltpu

D_MODEL = 1024
SEQ = 16384
HEAD_DIM = 64
FOX_HEADS = 8
SWA_Q_HEADS = 8
SWA_KV_HEADS = 2
SWA_GROUP = SWA_Q_HEADS // SWA_KV_HEADS
WINDOW = 128
D_FF = 2816
ROPE_THETA = 10000.0
RMS_EPS = 1e-6
FOX_WIDTH = FOX_HEADS * HEAD_DIM
SWA_Q_WIDTH = SWA_Q_HEADS * HEAD_DIM
SWA_KV_WIDTH = SWA_KV_HEADS * HEAD_DIM

LANES = 128
VMEM_LIMIT_BYTES = 56 * 1024 * 1024

ROW_TILE = 512
FF_CHUNK = 256
FOX_TQ = 1024
FOX_TK = 512
assert FOX_TQ == 2 * FOX_TK
FOX_C_LANES = 3 * FOX_HEADS
FOX_CPT_ROWS = 32
FOX_ONES_ROWS = 16
FOX_V_ROWS = HEAD_DIM + FOX_ONES_ROWS
SWA_ROWS = 512
MASKED = -1e30

F32 = jnp.float32
BF16 = jnp.bfloat16


def _rms(x, g):
    return x * lax.rsqrt(jnp.mean(x * x, axis=-1, keepdims=True) + RMS_EPS) * g


def _sigmoid(x):
    return 0.5 * (1.0 + jnp.tanh(0.5 * x))


def _dot(a, b):
    return jnp.dot(a, b, preferred_element_type=F32)


def _swiglu(xn, wg_ref, wu_ref, wd_ref):
    acc = None
    for c in range(D_FF // FF_CHUNK):
        sl = slice(c * FF_CHUNK, (c + 1) * FF_CHUNK)
        g = _dot(xn, wg_ref[:, sl])
        u = _dot(xn, wu_ref[:, sl])
        h = (g * _sigmoid(g) * u).astype(BF16)
        y = _dot(h, wd_ref[sl, :])
        acc = y if acc is None else acc + y
    return acc


def _const_spec(shape):
    return pl.BlockSpec(shape, lambda *_: (0,) * len(shape), pipeline_mode=pl.Buffered(1))


def _row_spec(cols, rows=ROW_TILE):
    return pl.BlockSpec((rows, cols), lambda i: (i, 0))


def _ffn1_kernel(x_ref, gpre_ref, gpost_ref, wg_ref, wu_ref, wd_ref, o_ref):
    x = x_ref[...]
    xn = _rms(x, gpre_ref[...]).astype(BF16)
    h = _swiglu(xn, wg_ref, wu_ref, wd_ref)
    o_ref[...] = x + 0.5 * _rms(h, gpost_ref[...])


def _ffn1(x, gpre, gpost, wg, wu, wd):
    return pl.pallas_call(
        _ffn1_kernel,
        grid=(SEQ // ROW_TILE,),
        in_specs=[
            _row_spec(D_MODEL),
            _const_spec((1, D_MODEL)),
            _const_spec((1, D_MODEL)),
            _const_spec((D_MODEL, D_FF)),
            _const_spec((D_MODEL, D_FF)),
            _const_spec((D_FF, D_MODEL)),
        ],
        out_specs=_row_spec(D_MODEL),
        out_shape=jax.ShapeDtypeStruct((SEQ, D_MODEL), F32),
        compiler_params=pltpu.CompilerParams(
            dimension_semantics=("arbitrary",), vmem_limit_bytes=VMEM_LIMIT_BYTES
        ),
        name="ffn1",
    )(x, gpre, gpost, wg, wu, wd)


def _split3(x):
    hi = x.astype(BF16)
    r = x - hi.astype(F32)
    mid = r.astype(BF16)
    lo = (r - mid.astype(F32)).astype(BF16)
    return hi, mid, lo


def _proj_kernel(
    x_ref, g_ref, wa_ref, wf_ref, fb_ref, wb_ref, wgate_ref, cos_ref, sin_ref,
    qat_ref, ka_ref, vat_ref, augk_ref, cpt_ref, qb_ref, kb_ref, vb_ref, ga_ref, gb_ref, carry_ref,
):
    @pl.when(pl.program_id(0) == 0)
    def _():
        carry_ref[...] = jnp.zeros_like(carry_ref)

    hn = _rms(x_ref[...], g_ref[...]).astype(BF16)

    pa = _dot(hn, wa_ref[...])
    qat_ref[...] = (pa[:, :FOX_WIDTH] * HEAD_DIM**-0.5).T.astype(BF16)
    ka_ref[...] = pa[:, FOX_WIDTH : 2 * FOX_WIDTH].astype(BF16)
    vat_ref[...] = pa[:, 2 * FOX_WIDTH :].T.astype(BF16)

    f = _dot(hn, wf_ref[...]) + fb_ref[...]
    log_f = jnp.minimum(f, 0.0) - jnp.log1p(jnp.exp(-jnp.abs(f)))
    hi, mid, lo = _split3(log_f)
    lane = lax.broadcasted_iota(jnp.int32, log_f.shape, 1)
    parts = jnp.where(lane < 8, hi, jnp.where(lane < 16, mid, jnp.where(lane < 24, lo, jnp.zeros_like(lo))))
    rows = log_f.shape[0]
    r_i = lax.broadcasted_iota(jnp.int32, (rows, rows), 0)
    c_i = lax.broadcasted_iota(jnp.int32, (rows, rows), 1)
    tril = jnp.where(r_i >= c_i, 1.0, 0.0).astype(BF16)
    y = _dot(tril, parts)
    csum = y + pltpu.roll(y, LANES - 8, 1) + pltpu.roll(y, LANES - 16, 1)
    c = jnp.where(lane < 8, csum + carry_ref[...], 0.0)
    carry_ref[...] = c[rows - 1 : rows, :]
    c_hi, c_mid, c_lo = _split3(c + pltpu.roll(c, 8, 1) + pltpu.roll(c, 16, 1))
    c_parts = jnp.where(lane < 8, c_hi, jnp.where(lane < 16, c_mid, c_lo)).astype(F32)
    augk_ref[...] = jnp.where(lane < FOX_C_LANES, -c_parts, jnp.where(lane < FOX_C_LANES + 3, 1.0, 0.0)).astype(BF16)
    cpt_ref[...] = c_parts.T[:FOX_CPT_ROWS, :]

    pb = _dot(hn, wb_ref[...])
    cos = cos_ref[...]
    sin = sin_ref[...]
    lane_b = lax.broadcasted_iota(jnp.int32, cos.shape, 1)
    first_half = (lane_b % HEAD_DIM) < (HEAD_DIM // 2)

    def rope(t):
        partner = jnp.where(first_half, pltpu.roll(t, LANES - 32, 1), pltpu.roll(t, 32, 1))
        return t * cos + partner * sin

    for j in range(SWA_Q_WIDTH // LANES):
        sl = slice(j * LANES, (j + 1) * LANES)
        qb_ref[:, sl] = (rope(pb[:, sl]) * HEAD_DIM**-0.5).astype(BF16)
    kb_ref[...] = rope(pb[:, SWA_Q_WIDTH : SWA_Q_WIDTH + SWA_KV_WIDTH]).astype(BF16)
    vb_ref[...] = pb[:, SWA_Q_WIDTH + SWA_KV_WIDTH :].astype(BF16)

    pg = _dot(hn, wgate_ref[...])
    sg = _sigmoid(pg)
    ga_ref[...] = sg[:, :D_MODEL].astype(BF16)
    gb_ref[...] = sg[:, D_MODEL:].astype(BF16)


def _proj(x1, g, wa, wf, fb, wb, wgate, cos, sin):
    assert ROW_TILE == FOX_TK
    cols_spec = lambda rows: pl.BlockSpec((rows, ROW_TILE), lambda i: (0, i))
    row_out = lambda cols: (_row_spec(cols), jax.ShapeDtypeStruct((SEQ, cols), BF16))
    outs = [
        (cols_spec(FOX_WIDTH), jax.ShapeDtypeStruct((FOX_WIDTH, SEQ), BF16)),
        row_out(FOX_WIDTH),
        (
            pl.BlockSpec((None, FOX_WIDTH, ROW_TILE), lambda i: (i, 0, 0)),
            jax.ShapeDtypeStruct((SEQ // ROW_TILE, FOX_WIDTH, ROW_TILE), BF16),
        ),
        row_out(LANES),
        (cols_spec(FOX_CPT_ROWS), jax.ShapeDtypeStruct((FOX_CPT_ROWS, SEQ), F32)),
        row_out(SWA_Q_WIDTH),
        row_out(SWA_KV_WIDTH),
        row_out(SWA_KV_WIDTH),
        row_out(D_MODEL),
        row_out(D_MODEL),
    ]
    return pl.pallas_call(
        _proj_kernel,
        grid=(SEQ // ROW_TILE,),
        in_specs=[
            _row_spec(D_MODEL),
            _const_spec((1, D_MODEL)),
            _const_spec(wa.shape),
            _const_spec(wf.shape),
            _const_spec(fb.shape),
            _const_spec(wb.shape),
            _const_spec(wgate.shape),
            _row_spec(LANES),
            _row_spec(LANES),
        ],
        out_specs=[spec for spec, _ in outs],
        out_shape=[shape for _, shape in outs],
        scratch_shapes=[pltpu.VMEM((1, LANES), F32)],
        compiler_params=pltpu.CompilerParams(
            dimension_semantics=("arbitrary",), vmem_limit_bytes=VMEM_LIMIT_BYTES
        ),
        name="proj",
    )(x1, g, wa, wf, fb, wb, wgate, cos, sin)


def _fox_kernel(
    qat_ref, cpt_ref, ka_ref, augk_ref, vt_ref, o_ref,
    qaug_ref, acc_ref, m_ref, s0_ref, s1_ref, mc0_ref, mc1_ref,
):
    h = pl.program_id(0)
    i = pl.program_id(1)
    n_full = (FOX_TQ // FOX_TK) * i
    acc_ref[...] = jnp.zeros_like(acc_ref)
    m_ref[...] = jnp.full_like(m_ref, MASKED)
    slots = ((s0_ref, mc0_ref), (s1_ref, mc1_ref))

    row = lax.broadcasted_iota(jnp.int32, (LANES, FOX_TQ), 0)
    q_pair = qat_ref[...].astype(F32)
    qaug_ref[:LANES, :] = jnp.where(row // HEAD_DIM == h % 2, q_pair, 0.0).astype(BF16)
    aug = jnp.where((row < FOX_C_LANES) & (row % FOX_HEADS == h), 1.0, 0.0)
    for p in range(3):
        aug = jnp.where(row == FOX_C_LANES + p, cpt_ref[pl.ds(p * FOX_HEADS + h, 1), :], aug)
    qaug_ref[LANES:, :] = aug.astype(BF16)
    ones_rows = jnp.ones((FOX_ONES_ROWS, FOX_TK), BF16)

    def scores(j, slot, c0=0, triangular=False):
        s_ref, mc_ref = slots[slot]
        keys = pl.ds(pl.multiple_of(j * FOX_TK, FOX_TK), FOX_TK)
        k_aug = jnp.concatenate([ka_ref[keys, :], augk_ref[keys, :]], axis=1)
        s = _dot(k_aug, qaug_ref[:, c0:])
        if triangular:
            kpos = lax.broadcasted_iota(jnp.int32, (FOX_TK, FOX_TK), 0)
            qpos = lax.broadcasted_iota(jnp.int32, (FOX_TK, FOX_TK), 1)
            tri = jnp.where(kpos <= qpos, s[:, :FOX_TK], MASKED)
            s_ref[:, c0 : c0 + FOX_TK] = tri
            mc_ref[:, c0 : c0 + FOX_TK] = jnp.max(tri, axis=0, keepdims=True)
            if c0 + FOX_TK < FOX_TQ:
                s_ref[:, c0 + FOX_TK :] = s[:, FOX_TK:]
                mc_ref[:, c0 + FOX_TK :] = jnp.max(s[:, FOX_TK:], axis=0, keepdims=True)
        else:
            s_ref[:, c0:] = s
            mc_ref[:, c0:] = jnp.max(s, axis=0, keepdims=True)

    def accumulate(j, slot, c0=0):
        s_ref, mc_ref = slots[slot]
        m_old = m_ref[:, c0:]
        m_new = jnp.maximum(m_old, mc_ref[:, c0:])
        alpha = jnp.exp(m_old - m_new)
        p = jnp.exp(s_ref[:, c0:] - m_new).astype(BF16)
        v_aug = jnp.concatenate([vt_ref[j], ones_rows], axis=0)
        acc_ref[:, c0:] = alpha * acc_ref[:, c0:] + _dot(v_aug, p)
        m_ref[:, c0:] = m_new

    def diagonal_tail(pending):
        scores(n_full, 0, 0, True)
        if pending is not None:
            accumulate(pending, 1)
        scores(n_full + 1, 1, FOX_TK, True)
        accumulate(n_full, 0)
        accumulate(n_full + 1, 1, FOX_TK)

    @pl.when(i == 0)
    def _():
        diagonal_tail(None)

    @pl.when(i > 0)
    def _():
        scores(0, 0)

        def pair(jj, carry):
            scores(2 * jj + 1, 1)
            accumulate(2 * jj, 0)
            scores(2 * jj + 2, 0)
            accumulate(2 * jj + 1, 1)
            return carry

        lax.fori_loop(0, n_full // 2 - 1, pair, 0)
        scores(n_full - 1, 1)
        accumulate(n_full - 2, 0)
        diagonal_tail(n_full - 1)

    acc = acc_ref[...]
    o_ref[...] = (acc[:HEAD_DIM, :] / acc[HEAD_DIM : HEAD_DIM + 1, :]).astype(o_ref.dtype)


def _fox(qat, cpt, ka, augk, vat):
    nk = SEQ // FOX_TK
    return pl.pallas_call(
        _fox_kernel,
        grid=(FOX_HEADS, SEQ // FOX_TQ),
        in_specs=[
            pl.BlockSpec((LANES, FOX_TQ), lambda h, i: (h // 2, i)),
            pl.BlockSpec((FOX_CPT_ROWS, FOX_TQ), lambda h, i: (0, i)),
            pl.BlockSpec((SEQ, LANES), lambda h, i: (0, h // 2)),
            pl.BlockSpec((SEQ, LANES), lambda h, i: (0, 0)),
            pl.BlockSpec((nk, HEAD_DIM, FOX_TK), lambda h, i: (0, h, 0)),
        ],
        out_specs=pl.BlockSpec((HEAD_DIM, FOX_TQ), lambda h, i: (h, i)),
        out_shape=jax.ShapeDtypeStruct((FOX_WIDTH, SEQ), BF16),
        scratch_shapes=[
            pltpu.VMEM((2 * LANES, FOX_TQ), BF16),
            pltpu.VMEM((FOX_V_ROWS, FOX_TQ), F32),
            pltpu.VMEM((1, FOX_TQ), F32),
            pltpu.VMEM((FOX_TK, FOX_TQ), F32),
            pltpu.VMEM((FOX_TK, FOX_TQ), F32),
            pltpu.VMEM((1, FOX_TQ), F32),
            pltpu.VMEM((1, FOX_TQ), F32),
        ],
        compiler_params=pltpu.CompilerParams(
            dimension_semantics=("arbitrary", "arbitrary"), vmem_limit_bytes=VMEM_LIMIT_BYTES
        ),
        name="fox",
    )(qat, cpt, ka, augk, vat)


def _swa_kernel(sink_ref, q_ref, k_ref, v_ref, kprev_ref, vprev_ref, o_ref):
    step_i = pl.program_id(0)
    q_i = lax.broadcasted_iota(jnp.int32, (WINDOW, 2 * WINDOW), 0) + WINDOW
    k_i = lax.broadcasted_iota(jnp.int32, (WINDOW, 2 * WINDOW), 1)
    diff = q_i - k_i
    in_window = (diff >= 0) & (diff < WINDOW)
    for b in range(SWA_ROWS // WINDOW):
        rows = slice(b * WINDOW, (b + 1) * WINDOW)
        if b == 0:
            k_prev, v_prev = kprev_ref[...], vprev_ref[...]
            mask = in_window & ((k_i >= WINDOW) | (step_i > 0))
        else:
            prev = slice((b - 1) * WINDOW, b * WINDOW)
            k_prev, v_prev = k_ref[prev, :], v_ref[prev, :]
            mask = in_window
        k_cat = jnp.concatenate([k_prev, k_ref[rows, :]], axis=0)
        v_cat = jnp.concatenate([v_prev, v_ref[rows, :]], axis=0)
        outs = []
        for h in range(SWA_Q_HEADS):
            g = h // SWA_GROUP
            q_h = q_ref[rows, h * HEAD_DIM : (h + 1) * HEAD_DIM]
            k_g = k_cat[:, g * HEAD_DIM : (g + 1) * HEAD_DIM]
            v_g = v_cat[:, g * HEAD_DIM : (g + 1) * HEAD_DIM]
            s = lax.dot_general(q_h, k_g, (((1,), (1,)), ((), ())), preferred_element_type=F32)
            s = jnp.where(mask, s, MASKED)
            sink = sink_ref[h]
            m = jnp.maximum(jnp.max(s, axis=-1, keepdims=True), sink)
            p = jnp.exp(s - m)
            denom = jnp.sum(p, axis=-1, keepdims=True) + jnp.exp(sink - m)
            outs.append(_dot(p.astype(BF16), v_g) / denom)
        o_ref[rows, :] = jnp.concatenate(outs, axis=-1).astype(o_ref.dtype)


def _swa(sinks, qb, kb, vb):
    blocks_per_step = SWA_ROWS // WINDOW
    prev_spec = pl.BlockSpec((WINDOW, SWA_KV_WIDTH), lambda i: (jnp.maximum(i * blocks_per_step - 1, 0), 0))
    return pl.pallas_call(
        _swa_kernel,
        grid=(SEQ // SWA_ROWS,),
        in_specs=[
            pl.BlockSpec(memory_space=pltpu.SMEM),
            _row_spec(SWA_Q_WIDTH, SWA_ROWS),
            _row_spec(SWA_KV_WIDTH, SWA_ROWS),
            _row_spec(SWA_KV_WIDTH, SWA_ROWS),
            prev_spec,
            prev_spec,
        ],
        out_specs=_row_spec(SWA_Q_WIDTH, SWA_ROWS),
        out_shape=jax.ShapeDtypeStruct((SEQ, SWA_Q_WIDTH), BF16),
        compiler_params=pltpu.CompilerParams(
            dimension_semantics=("arbitrary",), vmem_limit_bytes=VMEM_LIMIT_BYTES
        ),
        name="swa",
    )(sinks, qb, kb, vb, kb, vb)


def _out_kernel(
    x_ref, oat_ref, ob_ref, ga_ref, gb_ref, wfox_ref, wswa_ref, wout_ref,
    gmix_ref, gpre_ref, gpost_ref, wg_ref, wu_ref, wd_ref, o_ref,
):
    oa = oat_ref[...].astype(F32).T.astype(BF16)
    a = _dot(oa, wfox_ref[...])
    b = _dot(ob_ref[...], wswa_ref[...])
    merged = (ga_ref[...].astype(F32) * a + gb_ref[...].astype(F32) * b).astype(BF16)
    h = _dot(merged, wout_ref[...])
    x2 = x_ref[...] + _rms(h, gmix_ref[...])
    xn = _rms(x2, gpre_ref[...]).astype(BF16)
    h2 = _swiglu(xn, wg_ref, wu_ref, wd_ref)
    o_ref[...] = x2 + 0.5 * _rms(h2, gpost_ref[...])


def _out(x1, oat, ob, ga, gb, wfox, wswa, wout, gmix, gpre, gpost, wg, wu, wd):
    return pl.pallas_call(
        _out_kernel,
        grid=(SEQ // ROW_TILE,),
        in_specs=[
            _row_spec(D_MODEL),
            pl.BlockSpec((FOX_WIDTH, ROW_TILE), lambda i: (0, i)),
            _row_spec(SWA_Q_WIDTH),
            _row_spec(D_MODEL),
            _row_spec(D_MODEL),
            _const_spec(wfox.shape),
            _const_spec(wswa.shape),
            _const_spec(wout.shape),
            _const_spec((1, D_MODEL)),
            _const_spec((1, D_MODEL)),
            _const_spec((1, D_MODEL)),
            _const_spec((D_MODEL, D_FF)),
            _const_spec((D_MODEL, D_FF)),
            _const_spec((D_FF, D_MODEL)),
        ],
        out_specs=_row_spec(D_MODEL),
        out_shape=jax.ShapeDtypeStruct((SEQ, D_MODEL), F32),
        compiler_params=pltpu.CompilerParams(
            dimension_semantics=("arbitrary",), vmem_limit_bytes=VMEM_LIMIT_BYTES
        ),
        name="out_ffn2",
    )(x1, oat, ob, ga, gb, wfox, wswa, wout, gmix, gpre, gpost, wg, wu, wd)


def _rope_tables():
    inv_freq = 1.0 / (ROPE_THETA ** (jnp.arange(0, HEAD_DIM, 2, dtype=F32) / HEAD_DIM))
    ang = jnp.arange(SEQ, dtype=F32)[:, None] * inv_freq[None, :]
    cos, sin = jnp.cos(ang), jnp.sin(ang)
    reps = LANES // HEAD_DIM
    return jnp.tile(jnp.concatenate([cos, cos], axis=1), (1, reps)), jnp.tile(jnp.concatenate([-sin, sin], axis=1), (1, reps))


def kernel(x, ffn1_pre_g, ffn1_post_g, ffn1_w_gate, ffn1_w_up, ffn1_w_down, mix_pre_g, mix_post_g, w_in, fox_forget_bias, swa_sinks, w_branch_fox, w_branch_swa, w_out, ffn2_pre_g, ffn2_post_g, ffn2_w_gate, ffn2_w_up, ffn2_w_down):
    assert x.shape == (1, SEQ, D_MODEL) and ffn1_pre_g.shape[0] == 1
    bf = lambda w: w[0].astype(BF16)
    row = lambda g: g[0].reshape(1, -1)
    xs = x[0]

    x1 = _ffn1(xs, row(ffn1_pre_g), row(ffn1_post_g), bf(ffn1_w_gate), bf(ffn1_w_up), bf(ffn1_w_down))

    w = w_in[0]
    o_f = 3 * FOX_WIDTH
    o_b = o_f + FOX_HEADS
    o_g = o_b + SWA_Q_WIDTH + 2 * SWA_KV_WIDTH
    wa = w[:, :o_f].astype(BF16)
    wf3 = jnp.tile(w[:, o_f:o_b], (1, 3))
    wf = jnp.pad(wf3, ((0, 0), (0, LANES - 3 * FOX_HEADS))).astype(BF16)
    fb = jnp.pad(jnp.tile(fox_forget_bias[0], 3), (0, LANES - 3 * FOX_HEADS)).reshape(1, LANES)
    wb = w[:, o_b:o_g].astype(BF16)
    wgate = w[:, o_g:].astype(BF16)
    cos, sin = _rope_tables()
    qat, ka, vat, augk, cpt, qb, kb, vb, ga, gb = _proj(x1, row(mix_pre_g), wa, wf, fb, wb, wgate, cos, sin)

    oat = _fox(qat, cpt, ka, augk, vat)

    ob = _swa(swa_sinks[0], qb, kb, vb)

    out = _out(
        x1, oat, ob, ga, gb, bf(w_branch_fox), bf(w_branch_swa), bf(w_out),
        row(mix_post_g), row(ffn2_pre_g), row(ffn2_post_g), bf(ffn2_w_gate), bf(ffn2_w_up), bf(ffn2_w_down),
    )
    return out[None]
```
